```python
import math
import jax, jax.numpy as jnp
from jax import lax
import numpy as np

D_MODEL = 1024
BATCH = 8
SEQ = 2048
DEPTH = 2
DEC_BATCH = 32
DEC_SEQ = 1
PAST_LEN = 16384
PAGE_SIZE = 128

N_HEADS = 12
HEAD_DIM = 64
Q_LORA = 384
KV_LORA = 256
NOPE_DIM = 64
ROPE_DIM = 32
V_DIM = 64
KV_HEADS = 4
MOBA_BLOCK = 256
MOBA_TOPK = 3
MOBA_Q_CHUNK = 16
MEM_LEN = 256
MEM_HEADS = 4
MEM_HEAD_DIM = 64
D_FF = 2816
CONV_W = 3
ROPE_THETA = 10000.0
ATTN_Q_BLOCK = 128
LN_EPS = 1e-5
RMS_EPS = 1e-6
ALPHA = (2 * DEPTH) ** 0.25
BETA = (8 * DEPTH) ** -0.25
N_A = (DEPTH + 1) // 2
N_B = DEPTH // 2
SELF_WIDTH = N_HEADS * V_DIM
MEM_WIDTH = MEM_HEADS * MEM_HEAD_DIM
MIX_WIDTH = SELF_WIDTH + MEM_WIDTH
IN_A = Q_LORA + KV_LORA + ROPE_DIM + MEM_WIDTH
IN_B = N_HEADS * HEAD_DIM + 2 * KV_HEADS * HEAD_DIM + MEM_WIDTH

kernel_name = "mla_moba_memxattn_convffn_decode_step"


def layer_norm(x, g, b):
    xf = x.astype(jnp.float32)
    mu = jnp.mean(xf, -1, keepdims=True)
    var = jnp.mean(jnp.square(xf - mu), -1, keepdims=True)
    return ((xf - mu) * lax.rsqrt(var + LN_EPS) * g.astype(jnp.float32) + b.astype(jnp.float32)).astype(x.dtype)


def rms_norm(x, g):
    xf = x.astype(jnp.float32)
    ms = jnp.mean(jnp.square(xf), -1, keepdims=True)
    return (xf * lax.rsqrt(ms + RMS_EPS) * g.astype(jnp.float32)).astype(x.dtype)


def rope(x, pos):
    d = x.shape[-1]
    inv = ROPE_THETA ** (-jnp.arange(0, d, 2, dtype=jnp.float32) / d)
    ang = pos.astype(jnp.float32)[:, None] * inv[None, :]
    cos, sin = jnp.cos(ang), jnp.sin(ang)
    if x.ndim == 4:
        cos, sin = cos[:, None, :], sin[:, None, :]
    xf = x.astype(jnp.float32)
    x1, x2 = xf[..., : d // 2], xf[..., d // 2:]
    return jnp.concatenate([x1 * cos - x2 * sin, x2 * cos + x1 * sin], -1).astype(x.dtype)


def gather_pages(pool, page_table):
    g = pool[page_table]
    return g.reshape((g.shape[0], g.shape[1] * g.shape[2]) + g.shape[3:])


def map_query_blocks(fn, q_pos, xs, block):
    n_q = q_pos.shape[0]
    blk = block if n_q % block == 0 else n_q
    nb = n_q // blk
    split = lambda a: jnp.moveaxis(a.reshape((a.shape[0], nb, blk) + a.shape[2:]), 1, 0)
    out = lax.map(lambda a: fn(a[0], *a[1]), (q_pos.reshape(nb, blk), tuple(split(a) for a in xs)))
    out = jnp.moveaxis(out, 0, 1)
    return out.reshape((out.shape[0], n_q) + out.shape[3:])


def mla_attend(q_lat, q_rope, c, kr, q_pos):
    scale = (NOPE_DIM + ROPE_DIM) ** -0.5
    k_pos = jnp.arange(c.shape[1], dtype=jnp.int32)

    def block(pos, ql, qr):
        s = (jnp.einsum('bqhr,blr->bhql', ql, c) + jnp.einsum('bqhe,ble->bhql', qr, kr)).astype(jnp.float32) * scale
        s = jnp.where(k_pos[None, :] <= pos[:, None], s, -jnp.inf)
        p = jax.nn.softmax(s, axis=-1).astype(c.dtype)
        return jnp.einsum('bhql,blr->bqhr', p, c)

    return map_query_blocks(block, q_pos, (q_lat, q_rope), ATTN_Q_BLOCK)


def mla_mixer(h, pos, past_c, past_kr, g_q, w_q_b, g_kv, w_uk, w_uv):
    B, Q = h.shape[0], h.shape[1]
    q_a = h[..., :Q_LORA]
    c_kv = h[..., Q_LORA:Q_LORA + KV_LORA]
    k_r = h[..., Q_LORA + KV_LORA:Q_LORA + KV_LORA + ROPE_DIM]
    q = (rms_norm(q_a, g_q) @ w_q_b).reshape(B, Q, N_HEADS, NOPE_DIM + ROPE_DIM)
    q_nope = q[..., :NOPE_DIM]
    q_rope = rope(q[..., NOPE_DIM:], pos)
    c_new = rms_norm(c_kv, g_kv)
    kr_new = rope(k_r, pos)
    c_all = c_new if past_c is None else jnp.concatenate([past_c, c_new], axis=1)
    kr_all = kr_new if past_kr is None else jnp.concatenate([past_kr, kr_new], axis=1)
    q_lat = jnp.einsum('bqhd,rhd->bqhr', q_nope, w_uk)
    o_lat = mla_attend(q_lat, q_rope, c_all, kr_all, pos)
    o = jnp.einsum('bqhr,rhd->bqhd', o_lat, w_uv).reshape(B, Q, SELF_WIDTH)
    return o, c_new, kr_new


def moba_attend(q, k, v, q_pos):
    B, L = k.shape[0], k.shape[1]
    G = N_HEADS // KV_HEADS
    nb = -(-L // MOBA_BLOCK)
    pad = ((0, 0), (0, nb * MOBA_BLOCK - L), (0, 0), (0, 0))
    kb = jnp.pad(k, pad).reshape(B, nb, MOBA_BLOCK, KV_HEADS, HEAD_DIM).transpose(0, 3, 1, 2, 4)
    vb = jnp.pad(v, pad).reshape(B, nb, MOBA_BLOCK, KV_HEADS, HEAD_DIM).transpose(0, 3, 1, 2, 4)
    k_mean = jnp.mean(kb.astype(jnp.float32), axis=3)
    qg = q.reshape(B, q.shape[1], KV_HEADS, G, HEAD_DIM)
    gate = jnp.einsum('bqkgd,bknd->bkgqn', qg.astype(jnp.float32), k_mean)
    own = q_pos // MOBA_BLOCK
    gate = jnp.where(jnp.arange(nb)[None, :] < own[:, None], gate, -jnp.inf)
    top_val, top_idx = lax.top_k(gate, min(MOBA_TOPK, nb))
    sel_ok = jnp.isfinite(top_val)
    own_idx = jnp.broadcast_to(own[:, None], top_idx.shape[:-1] + (1,)).astype(top_idx.dtype)
    idx = jnp.moveaxis(jnp.concatenate([top_idx, own_idx], -1), 3, 1)
    ok = jnp.moveaxis(jnp.concatenate([sel_ok, jnp.ones(own_idx.shape, bool)], -1), 3, 1)
    bi = jnp.arange(B)[:, None, None, None, None]
    hi = jnp.arange(KV_HEADS)[None, None, :, None, None]
    scale = HEAD_DIM ** -0.5

    def chunk(pos, qc, ic, oc):
        kg = kb[bi, hi, ic]
        vg = vb[bi, hi, ic]
        kpos = ic[..., None] * MOBA_BLOCK + jnp.arange(MOBA_BLOCK)
        mask = oc[..., None] & (kpos <= pos[None, :, None, None, None, None])
        s = jnp.einsum('bckgd,bckgnjd->bckgnj', qc, kg).astype(jnp.float32) * scale
        s = jnp.where(mask, s, -jnp.inf)
        p = jax.nn.softmax(s.reshape(s.shape[:4] + (-1,)), axis=-1).reshape(s.shape).astype(vg.dtype)
        return jnp.einsum('bckgnj,bckgnjd->bckgd', p, vg)

    o = map_query_blocks(chunk, q_pos, (qg, idx, ok), MOBA_Q_CHUNK)
    return o.reshape(B, q.shape[1], N_HEADS * HEAD_DIM)


def moba_mixer(h, pos, past_k, past_v):
    B, Q = h.shape[0], h.shape[1]
    nq, nk = N_HEADS * HEAD_DIM, KV_HEADS * HEAD_DIM
    q = rope(h[..., :nq].reshape(B, Q, N_HEADS, HEAD_DIM), pos)
    k = rope(h[..., nq:nq + nk].reshape(B, Q, KV_HEADS, HEAD_DIM), pos)
    v = h[..., nq + nk:nq + 2 * nk].reshape(B, Q, KV_HEADS, HEAD_DIM)
    k_all = k if past_k is None else jnp.concatenate([past_k, k], axis=1)
    v_all = v if past_v is None else jnp.concatenate([past_v, v], axis=1)
    return moba_attend(q, k_all, v_all, pos), k, v


def mem_attend(qm, mk, mv):
    s = jnp.einsum('bqhd,bmhd->bhqm', qm, mk).astype(jnp.float32) * (MEM_HEAD_DIM ** -0.5)
    p = jax.nn.softmax(s, axis=-1).astype(mv.dtype)
    return jnp.einsum('bhqm,bmhd->bqhd', p, mv)


def conv_ffn(x, conv_state, w_up, conv_w, conv_b, w_down):
    Q = x.shape[1]
    ug = x @ w_up
    u, g = ug[..., :D_FF], ug[..., D_FF:]
    g_ext = jnp.concatenate([conv_state.astype(g.dtype), g], axis=1)
    gc = conv_b
    for t in range(CONV_W):
        gc = gc + conv_w[t] * g_ext[:, t:t + Q]
    h = jax.nn.gelu(gc, approximate=False) * u
    return h @ w_down, g_ext[:, -(CONV_W - 1):]


def setup_inputs(seed: int = 0) -> dict:
    key = jax.random.key(seed)
    ks = iter(jax.random.split(key, 40))
    nrm = lambda shape, s=1.0: jax.random.normal(next(ks), shape, jnp.float32) * s
    n_pages = PAST_LEN // PAGE_SIZE
    n_pool = (5 * DEC_BATCH * n_pages + 3) // 4
    x_prompt = nrm((BATCH, SEQ, D_MODEL))
    x_sample = nrm((DEC_BATCH, DEC_SEQ, D_MODEL))
    cache_mla_ckv = nrm((N_A, n_pool, PAGE_SIZE, KV_LORA))
    cache_mla_krope = nrm((N_A, n_pool, PAGE_SIZE, ROPE_DIM))
    cache_moba_k = nrm((N_B, n_pool, PAGE_SIZE, KV_HEADS, HEAD_DIM))
    cache_moba_v = nrm((N_B, n_pool, PAGE_SIZE, KV_HEADS, HEAD_DIM))
    cache_mem_k = nrm((DEPTH, DEC_BATCH, MEM_LEN, MEM_HEADS, MEM_HEAD_DIM))
    cache_mem_v = nrm((DEPTH, DEC_BATCH, MEM_LEN, MEM_HEADS, MEM_HEAD_DIM))
    state_conv = nrm((DEPTH, DEC_BATCH, CONV_W - 1, D_FF))
    perm = jax.random.permutation(next(ks), n_pool)
    page_table = perm[:DEC_BATCH * n_pages].reshape(DEC_BATCH, n_pages).astype(jnp.int32)
    mem_prompt = nrm((BATCH, MEM_LEN, D_MODEL))
    return {
        "x_prompt": x_prompt, "x_sample": x_sample,
        "cache_mla_ckv": cache_mla_ckv, "cache_mla_krope": cache_mla_krope,
        "cache_moba_k": cache_moba_k, "cache_moba_v": cache_moba_v,
        "cache_mem_k": cache_mem_k, "cache_mem_v": cache_mem_v,
        "state_conv": state_conv, "page_table": page_table, "mem_prompt": mem_prompt,
        "w_in_a": nrm((N_A, D_MODEL, IN_A), D_MODEL ** -0.5),
        "g_q": 1.0 + nrm((N_A, Q_LORA), 0.01),
        "w_q_b": nrm((N_A, Q_LORA, N_HEADS * (NOPE_DIM + ROPE_DIM)), Q_LORA ** -0.5),
        "g_kv": 1.0 + nrm((N_A, KV_LORA), 0.01),
        "w_uk": nrm((N_A, KV_LORA, N_HEADS, NOPE_DIM), KV_LORA ** -0.5),
        "w_uv": nrm((N_A, KV_LORA, N_HEADS, V_DIM), KV_LORA ** -0.5),
        "w_in_b": nrm((N_B, D_MODEL, IN_B), D_MODEL ** -0.5),
        "w_mem_k": nrm((DEPTH, D_MODEL, MEM_WIDTH), D_MODEL ** -0.5),
        "w_mem_v": nrm((DEPTH, D_MODEL, MEM_WIDTH), D_MODEL ** -0.5),
        "w_o": nrm((DEPTH, MIX_WIDTH, D_MODEL), BETA * MIX_WIDTH ** -0.5),
        "ln1_g": 1.0 + nrm((DEPTH, D_MODEL), 0.01),
        "ln1_b": nrm((DEPTH, D_MODEL), 0.01),
        "w_up": nrm((DEPTH, D_MODEL, 2 * D_FF), D_MODEL ** -0.5),
        "conv_w": nrm((DEPTH, CONV_W, D_FF), CONV_W ** -0.5),
        "conv_b": nrm((DEPTH, D_FF), 0.01),
        "w_down": nrm((DEPTH, D_FF, D_MODEL), BETA * D_FF ** -0.5),
        "ln2_g": 1.0 + nrm((DEPTH, D_MODEL), 0.01),
        "ln2_b": nrm((DEPTH, D_MODEL), 0.01),
    }


def reference(x_prompt, x_sample, cache_mla_ckv, cache_mla_krope, cache_moba_k, cache_moba_v,
              cache_mem_k, cache_mem_v, state_conv, page_table, mem_prompt,
              w_in_a, g_q, w_q_b, g_kv, w_uk, w_uv, w_in_b, w_mem_k, w_mem_v, w_o,
              ln1_g, ln1_b, w_up, conv_w, conv_b, w_down, ln2_g, ln2_b):
    pos_p = jnp.arange(SEQ, dtype=jnp.int32)
    pos_s = PAST_LEN + jnp.arange(DEC_SEQ, dtype=jnp.int32)

    def layer(i, x, pos, mk, mv, past, conv_state):
        j = i // 2
        B, Q = x.shape[0], x.shape[1]
        if i % 2 == 0:
            proj = x @ w_in_a[j]
            self_out, r0, r1 = mla_mixer(proj[..., :IN_A - MEM_WIDTH], pos, past[0], past[1],
                                         g_q[j], w_q_b[j], g_kv[j], w_uk[j], w_uv[j])
        else:
            proj = x @ w_in_b[j]
            self_out, r0, r1 = moba_mixer(proj[..., :IN_B - MEM_WIDTH], pos, past[0], past[1])
        qm = proj[..., -MEM_WIDTH:].reshape(B, Q, MEM_HEADS, MEM_HEAD_DIM)
        mem_out = mem_attend(qm, mk, mv).reshape(B, Q, MEM_WIDTH)
        mix = jnp.concatenate([self_out, mem_out], axis=-1) @ w_o[i]
        x = layer_norm(ALPHA * x + mix, ln1_g[i], ln1_b[i])
        f, conv_new = conv_ffn(x, conv_state, w_up[i], conv_w[i], conv_b[i], w_down[i])
        x = layer_norm(ALPHA * x + f, ln2_g[i], ln2_b[i])
        return x, r0, r1, conv_new

    xp, xs = x_prompt, x_sample
    rows_p = ([], [])
    rows_s = ([], [])
    memk_p, memv_p, conv_p, conv_s = [], [], [], []
    for i in range(DEPTH):
        j = i // 2
        Bp = mem_prompt.shape[0]
        mk = (mem_prompt @ w_mem_k[i]).reshape(Bp, MEM_LEN, MEM_HEADS, MEM_HEAD_DIM)
        mv = (mem_prompt @ w_mem_v[i]).reshape(Bp, MEM_LEN, MEM_HEADS, MEM_HEAD_DIM)
        zero_conv = jnp.zeros((xp.shape[0], CONV_W - 1, D_FF), xp.dtype)
        xp, r0, r1, cp = layer(i, xp, pos_p, mk, mv, (None, None), zero_conv)
        if i % 2 == 0:
            past_s = (gather_pages(cache_mla_ckv[j], page_table), gather_pages(cache_mla_krope[j], page_table))
        else:
            past_s = (gather_pages(cache_moba_k[j], page_table), gather_pages(cache_moba_v[j], page_table))
        xs, s0, s1, cs = layer(i, xs, pos_s, cache_mem_k[i], cache_mem_v[i], past_s, state_conv[i])
        rows_p[i % 2].append((r0, r1))
        rows_s[i % 2].append((s0, s1))
        memk_p.append(mk)
        memv_p.append(mv)
        conv_p.append(cp)
        conv_s.append(cs)

    ckv_p = jnp.stack([r[0] for r in rows_p[0]])
    krope_p = jnp.stack([r[1] for r in rows_p[0]])
    mobak_p = jnp.stack([r[0] for r in rows_p[1]])
    mobav_p = jnp.stack([r[1] for r in rows_p[1]])
    ckv_s = jnp.stack([r[0] for r in rows_s[0]])
    krope_s = jnp.stack([r[1] for r in rows_s[0]])
    mobak_s = jnp.stack([r[0] for r in rows_s[1]])
    mobav_s = jnp.stack([r[1] for r in rows_s[1]])
    memk_out = jnp.stack(memk_p)
    memv_out = jnp.stack(memv_p)
    conv_p_out = jnp.stack(conv_p)
    conv_s_out = jnp.stack(conv_s)
    return (xp, xs, ckv_p, krope_p, mobak_p, mobav_p, memk_out, memv_out, conv_p_out,
            ckv_s, krope_s, mobak_s, mobav_s, conv_s_out)
```

```python
import functools

import jax
import jax.numpy as jnp
from jax import lax
from jax.experimental import pallas as pl
from jax.experimental.pallas import tpu as pltpu

F32, BF16, I32 = jnp.float32, jnp.bfloat16, jnp.int32

D_MODEL = 1024
N_HEADS = 12
HEAD_DIM = 64
Q_LORA = 384
KV_LORA = 256
NOPE_DIM = 64
ROPE_DIM = 32
V_DIM = 64
KV_HEADS = 4
MOBA_BLOCK = 256
MOBA_TOPK = 3
MEM_HEADS = 4
MEM_HEAD_DIM = 64
D_FF = 2816
CONV_W = 3
DEPTH = 2
ROPE_THETA = 10000.0
LN_EPS = 1e-5
RMS_EPS = 1e-6
ALPHA = (2 * DEPTH) ** 0.25
SELF_WIDTH = N_HEADS * V_DIM
MEM_WIDTH = MEM_HEADS * MEM_HEAD_DIM
MLA_SCALE = (NOPE_DIM + ROPE_DIM) ** -0.5
HEAD_SCALE = HEAD_DIM ** -0.5
LANES = 128
HALF = 64
HEADS_PAD = 16
NEG = -1e30
MOBA_PERM = (0, 3, 1, 4, 2, 5, 6, 9, 7, 10, 8, 11)
ROW_TILE = 512
ATTN_TILE = 256
FF_TILE = 1408
PAGES_PER_STEP = 16


def _mm(a, b):
    return jnp.dot(a, b, preferred_element_type=F32)


def _mm_nt(a, b, precision=None):
    return lax.dot_general(a, b, (((1,), (1,)), ((), ())), preferred_element_type=F32, precision=precision)


def _rms(x, g):
    ms = jnp.mean(x * x, axis=-1, keepdims=True)
    return x * lax.rsqrt(ms + RMS_EPS) * g


def _layer_norm(x, g, b):
    mu = jnp.mean(x, axis=-1, keepdims=True)
    xc = x - mu
    var = jnp.mean(xc * xc, axis=-1, keepdims=True)
    return xc * lax.rsqrt(var + LN_EPS) * g + b


def _softmax_start(s):
    m = jnp.max(s, axis=1, keepdims=True)
    p = jnp.exp(s - m)
    return m, jnp.sum(p, axis=1, keepdims=True), p


def _softmax_step(s, m, l):
    m_new = jnp.maximum(m, jnp.max(s, axis=1, keepdims=True))
    a = jnp.exp(m - m_new)
    p = jnp.exp(s - m_new)
    return m_new, a * l + jnp.sum(p, axis=1, keepdims=True), a, p


def _mem_proj_kernel(x_ref, w_ref, k32_ref, v32_ref, kb_ref, vb_ref):
    r = _mm(x_ref[...].astype(BF16), w_ref[...])
    for l in range(DEPTH):
        k = r[:, (2 * l) * MEM_WIDTH:(2 * l + 1) * MEM_WIDTH]
        v = r[:, (2 * l + 1) * MEM_WIDTH:(2 * l + 2) * MEM_WIDTH]
        k32_ref[l] = k
        v32_ref[l] = v
        kb_ref[l] = k.astype(BF16)
        vb_ref[l] = v.astype(BF16)


def _mem_proj(mem, w):
    rows = mem.shape[0]
    tm = min(ROW_TILE, rows)
    blk = pl.BlockSpec((DEPTH, tm, MEM_WIDTH), lambda i: (0, i, 0))
    shp = (DEPTH, rows, MEM_WIDTH)
    return pl.pallas_call(
        _mem_proj_kernel,
        grid=(rows // tm,),
        in_specs=[pl.BlockSpec((tm, D_MODEL), lambda i: (i, 0)),
                  pl.BlockSpec(w.shape, lambda i: (0, 0))],
        out_specs=[blk, blk, blk, blk],
        out_shape=[jax.ShapeDtypeStruct(shp, F32), jax.ShapeDtypeStruct(shp, F32),
                   jax.ShapeDtypeStruct(shp, BF16), jax.ShapeDtypeStruct(shp, BF16)],
        compiler_params=pltpu.CompilerParams(dimension_semantics=("parallel",)),
        name="mem_proj",
    )(mem, w)


def _pair_attend(qblk, kblk, vblk, lane):
    outs = []
    q32 = qblk.astype(F32)
    for half in range(2):
        in_half = (lane >= HALF) if half else (lane < HALF)
        qh = jnp.where(in_half, q32, 0.0).astype(BF16)
        s = _mm_nt(qh, kblk)
        _, l, p = _softmax_start(s)
        outs.append(_mm(p.astype(BF16), vblk) / l)
    return jnp.where(lane < HALF, outs[0], outs[1])


def _mem_attn_kernel(q_ref, k_ref, v_ref, o_ref):
    tq = q_ref.shape[0]
    lane = lax.broadcasted_iota(I32, (tq, LANES), 1)
    for pair in range(MEM_WIDTH // LANES):
        cs = slice(pair * LANES, (pair + 1) * LANES)
        o = _pair_attend(q_ref[:, cs], k_ref[0, :, cs], v_ref[0, :, cs], lane)
        o_ref[:, cs] = o.astype(BF16)


def _mem_attn(qm, kb, vb, layer, batch, mem_len):
    rows = qm.shape[0]
    seq = rows // batch
    tq = min(ROW_TILE, seq)
    nq = seq // tq
    kv_spec = pl.BlockSpec((1, mem_len, MEM_WIDTH), lambda b, i: (layer, b, 0))
    return pl.pallas_call(
        _mem_attn_kernel,
        grid=(batch, nq),
        in_specs=[pl.BlockSpec((tq, MEM_WIDTH), lambda b, i: (b * nq + i, 0)), kv_spec, kv_spec],
        out_specs=pl.BlockSpec((tq, MEM_WIDTH), lambda b, i: (b * nq + i, 0)),
        out_shape=jax.ShapeDtypeStruct((rows, MEM_WIDTH), BF16),
        compiler_params=pltpu.CompilerParams(dimension_semantics=("parallel", "parallel")),
        name="mem_attn",
    )(qm, kb, vb)


def _mem_attn_s_kernel(q_ref, k_ref, v_ref, o_ref):
    q = q_ref[0].astype(F32)
    sub = lax.broadcasted_iota(I32, (8, MEM_WIDTH), 0)
    lane = lax.broadcasted_iota(I32, (8, MEM_WIDTH), 1)
    own = (lane // MEM_HEAD_DIM) == sub
    qb = jnp.where(own, jnp.broadcast_to(q, (8, MEM_WIDTH)), 0.0).astype(BF16)
    s = _mm_nt(qb, k_ref[0, 0].astype(BF16))
    _, l, p = _softmax_start(s)
    o = _mm(p.astype(BF16), v_ref[0, 0].astype(BF16)) / l
    o_ref[0] = jnp.sum(jnp.where(own, o, 0.0), axis=0, keepdims=True).astype(BF16)


def _mem_attn_s(qm, cache_k, cache_v, layer):
    bd = qm.shape[0]
    mem_len = cache_k.shape[2]
    kv_spec = pl.BlockSpec((1, 1, mem_len, MEM_WIDTH), lambda b: (layer, b, 0, 0))
    out = pl.pallas_call(
        _mem_attn_s_kernel,
        grid=(bd,),
        in_specs=[pl.BlockSpec((1, 1, MEM_WIDTH), lambda b: (b, 0, 0)), kv_spec, kv_spec],
        out_specs=pl.BlockSpec((1, 1, MEM_WIDTH), lambda b: (b, 0, 0)),
        out_shape=jax.ShapeDtypeStruct((bd, 1, MEM_WIDTH), BF16),
        compiler_params=pltpu.CompilerParams(dimension_semantics=("parallel",)),
        name="mem_attn_s",
    )(qm.reshape(bd, 1, MEM_WIDTH), cache_k, cache_v)
    return out.reshape(bd, MEM_WIDTH)


_A_QA = slice(0, Q_LORA)
_A_CKV = slice(Q_LORA, Q_LORA + KV_LORA)
_A_KR = slice(640, 768)
_A_KRS = slice(768, 896)
_A_QM = slice(896, 1152)
MLA_QW = N_HEADS * LANES


def _mla_front(x_ref, wa_ref, wq_ref, gq_ref, gkv_ref, cq_ref, sq_ref, ckr_ref, skr_ref):
    p = _mm(x_ref[...].astype(BF16), wa_ref[...])
    c_new = _rms(p[:, _A_CKV], gkv_ref[...])
    kr_full = p[:, _A_KR] * ckr_ref[...] + p[:, _A_KRS] * skr_ref[...]
    qn = _rms(p[:, _A_QA], gq_ref[...]).astype(BF16)
    qq = _mm(qn, wq_ref[...])
    cq, sq = cq_ref[...], sq_ref[...]
    q_heads = []
    for h in range(N_HEADS):
        a = qq[:, h * LANES:(h + 1) * LANES]
        b = qq[:, MLA_QW + h * LANES:MLA_QW + (h + 1) * LANES]
        q_heads.append((a * cq + b * sq).astype(BF16))
    return p, c_new, kr_full, q_heads


def _proj_mla_kernel(x_ref, wa_ref, wq_ref, wk_ref, wv_ref, gq_ref, gkv_ref, cq_ref, sq_ref, ckr_ref, skr_ref,
                     q_ref, k_ref, v_ref, qm_ref, ckv_ref, kr_ref):
    p, c_new, kr_full, q_heads = _mla_front(x_ref, wa_ref, wq_ref, gq_ref, gkv_ref, cq_ref, sq_ref, ckr_ref, skr_ref)
    for h in range(N_HEADS):
        q_ref[:, h * LANES:(h + 1) * LANES] = q_heads[h]
    ckv_ref[...] = c_new
    kr_ref[...] = kr_full[:, :ROPE_DIM]
    cb = c_new.astype(BF16)
    kin = jnp.concatenate([cb, kr_full.astype(BF16)], axis=1)
    k_ref[...] = _mm(kin, wk_ref[...]).astype(BF16)
    v_ref[...] = _mm(cb, wv_ref[...]).astype(BF16)
    qm_ref[...] = p[:, _A_QM].astype(BF16)


def _proj_mla(x, w, tabs, seq):
    rows = x.shape[0]
    tm = min(ROW_TILE, seq)
    nt = seq // tm
    full = lambda a: pl.BlockSpec(a.shape, lambda i: (0,) * a.ndim)
    tab = pl.BlockSpec((tm, LANES), lambda i: (i % nt, 0))
    row = lambda n: pl.BlockSpec((tm, n), lambda i: (i, 0))
    return pl.pallas_call(
        _proj_mla_kernel,
        grid=(rows // tm,),
        in_specs=[row(D_MODEL), full(w["wa"]), full(w["wq"]), full(w["wk"]), full(w["wv"]),
                  full(w["gq"]), full(w["gkv"]), tab, tab, tab, tab],
        out_specs=[row(MLA_QW), row(MLA_QW), row(SELF_WIDTH), row(MEM_WIDTH), row(KV_LORA), row(ROPE_DIM)],
        out_shape=[jax.ShapeDtypeStruct((rows, MLA_QW), BF16), jax.ShapeDtypeStruct((rows, MLA_QW), BF16),
                   jax.ShapeDtypeStruct((rows, SELF_WIDTH), BF16), jax.ShapeDtypeStruct((rows, MEM_WIDTH), BF16),
                   jax.ShapeDtypeStruct((rows, KV_LORA), F32), jax.ShapeDtypeStruct((rows, ROPE_DIM), F32)],
        compiler_params=pltpu.CompilerParams(dimension_semantics=("parallel",)),
        name="proj_mla",
    )(x, w["wa"], w["wq"], w["wk"], w["wv"], w["gq"], w["gkv"], tabs["cq"], tabs["sq"], tabs["ckr"], tabs["skr"])


def _proj_mla_s_kernel(x_ref, wa_ref, wq_ref, wukt_ref, psel_ref, gq_ref, gkv_ref, cq_ref, sq_ref, ckr_ref, skr_ref,
                       qlat_ref, qrope_ref, qm_ref, ckv_ref, kr_ref):
    p, c_new, kr_full, q_heads = _mla_front(x_ref, wa_ref, wq_ref, gq_ref, gkv_ref, cq_ref, sq_ref, ckr_ref, skr_ref)
    for h in range(N_HEADS):
        qlat_ref[h] = _mm(q_heads[h], wukt_ref[h]).astype(BF16)
        qrope_ref[h] = _mm(q_heads[h], psel_ref[...])[:, :ROPE_DIM].astype(BF16)
    for h in range(N_HEADS, HEADS_PAD):
        qlat_ref[h] = jnp.zeros(qlat_ref.shape[1:], BF16)
        qrope_ref[h] = jnp.zeros(qrope_ref.shape[1:], BF16)
    ckv_ref[...] = c_new
    kr_ref[...] = kr_full[:, :ROPE_DIM]
    qm_ref[...] = p[:, _A_QM].astype(BF16)


def _proj_mla_s(x, w, tabs):
    bd = x.shape[0]
    ins = (x, w["wa"], w["wq"], w["wukt"], w["psel"], w["gq"], w["gkv"],
           tabs["cq"], tabs["sq"], tabs["ckr"], tabs["skr"])
    return pl.pallas_call(
        _proj_mla_s_kernel,
        out_shape=[jax.ShapeDtypeStruct((HEADS_PAD, bd, KV_LORA), BF16),
                   jax.ShapeDtypeStruct((HEADS_PAD, bd, ROPE_DIM), BF16),
                   jax.ShapeDtypeStruct((bd, MEM_WIDTH), BF16),
                   jax.ShapeDtypeStruct((bd, KV_LORA), F32), jax.ShapeDtypeStruct((bd, ROPE_DIM), F32)],
        name="proj_mla_s",
    )(*ins)


def _mla_attn_kernel(q_ref, k_ref, v_ref, o_ref):
    i = pl.program_id(2)
    t = q_ref.shape[0]
    lane = lax.broadcasted_iota(I32, (t, LANES), 1)
    causal = lax.broadcasted_iota(I32, (t, t), 1) <= lax.broadcasted_iota(I32, (t, t), 0)
    outs = []
    for hh in range(2):
        cs = slice(hh * LANES, (hh + 1) * LANES)
        q = q_ref[:, cs]

        def tile(j):
            r0 = pl.multiple_of(j * t, t)
            return _mm_nt(q, k_ref[pl.ds(r0, t), cs]), v_ref[pl.ds(r0, t), :]

        s, v = tile(i)
        m, l, p = _softmax_start(jnp.where(causal, s, NEG))
        acc = _mm(p.astype(BF16), v)

        def body(j, carry):
            m, l, acc = carry
            s, v = tile(j)
            m, l, a, p = _softmax_step(s, m, l)
            return m, l, a * acc + _mm(p.astype(BF16), v)

        m, l, acc = lax.fori_loop(0, i, body, (m, l, acc))
        outs.append(acc / l)
    o_ref[...] = jnp.where(lane < HALF, outs[0], outs[1]).astype(BF16)


def _mla_attn(q, k, v, batch, seq):
    rows = q.shape[0]
    t = min(ATTN_TILE, seq)
    nq = seq // t
    return pl.pallas_call(
        _mla_attn_kernel,
        grid=(batch, N_HEADS // 2, nq),
        in_specs=[pl.BlockSpec((t, 2 * LANES), lambda b, hp, i: (b * nq + i, hp)),
                  pl.BlockSpec((seq, 2 * LANES), lambda b, hp, i: (b, hp)),
                  pl.BlockSpec((seq, LANES), lambda b, hp, i: (b, hp))],
        out_specs=pl.BlockSpec((t, LANES), lambda b, hp, i: (b * nq + i, hp)),
        out_shape=jax.ShapeDtypeStruct((rows, SELF_WIDTH), BF16),
        compiler_params=pltpu.CompilerParams(dimension_semantics=("parallel", "parallel", "arbitrary")),
        name="mla_attn",
    )(q, k, v)


def _mla_decode_kernel(pt_ref, qlat_ref, qrope_ref, cnew_ref, krnew_ref, *rest, pages):
    del pt_ref
    c_refs, kr_refs = rest[:pages], rest[pages:2 * pages]
    o_ref, m_ref, l_ref, acc_ref = rest[2 * pages:]
    step = pl.program_id(1)

    @pl.when(step == 0)
    def _():
        m_ref[...] = jnp.full(m_ref.shape, NEG, F32)
        l_ref[...] = jnp.zeros(l_ref.shape, F32)
        acc_ref[...] = jnp.zeros(acc_ref.shape, F32)

    ql = qlat_ref[0]
    qr = qrope_ref[0]
    cs, ss = [], []
    for i in range(pages):
        c = c_refs[i][0].astype(BF16)
        cs.append(c)
        ss.append(_mm_nt(ql, c) + _mm_nt(qr, kr_refs[i][0].astype(BF16)))
    s = jnp.concatenate(ss, axis=1)
    m, l, a, p = _softmax_step(s, m_ref[...], l_ref[...])
    page = cs[0].shape[0]
    pv = _mm(p[:, :page].astype(BF16), cs[0])
    for i in range(1, pages):
        pv = pv + _mm(p[:, i * page:(i + 1) * page].astype(BF16), cs[i])
    acc = a * acc_ref[...] + pv
    m_ref[...] = m
    l_ref[...] = l
    acc_ref[...] = acc

    @pl.when(step == pl.num_programs(1) - 1)
    def _():
        cn = cnew_ref[0].astype(BF16).astype(F32)
        kn = krnew_ref[0].astype(BF16).astype(F32)
        s_new = (jnp.sum(ql.astype(F32) * cn, axis=1, keepdims=True)
                 + jnp.sum(qr.astype(F32) * kn, axis=1, keepdims=True))
        m_f = jnp.maximum(m, s_new)
        a_f = jnp.exp(m - m_f)
        p_new = jnp.exp(s_new - m_f)
        l_f = a_f * l + p_new
        o_ref[0] = (a_f * acc + p_new.astype(BF16).astype(F32) * cn) / l_f


def _mla_decode(page_table, qlat, qrope, c_new, kr_new, pool_c, pool_kr):
    bd, n_pages = page_table.shape
    page = pool_c.shape[1]
    pp = _pages_per_step(n_pages)

    def pmap(b, s, pt, i):
        return (pt[b, s * pp + i], 0, 0)

    fixed = lambda shape: pl.BlockSpec((1,) + shape, lambda b, s, pt: (b, 0, 0))
    in_specs = ([fixed((HEADS_PAD, KV_LORA)), fixed((HEADS_PAD, ROPE_DIM)), fixed((1, KV_LORA)), fixed((1, ROPE_DIM))]
                + [pl.BlockSpec((1, page, KV_LORA), functools.partial(pmap, i=i)) for i in range(pp)]
                + [pl.BlockSpec((1, page, ROPE_DIM), functools.partial(pmap, i=i)) for i in range(pp)])
    return pl.pallas_call(
        functools.partial(_mla_decode_kernel, pages=pp),
        grid_spec=pltpu.PrefetchScalarGridSpec(
            num_scalar_prefetch=1, grid=(bd, n_pages // pp), in_specs=in_specs,
            out_specs=fixed((HEADS_PAD, KV_LORA)),
            scratch_shapes=[pltpu.VMEM((HEADS_PAD, 1), F32), pltpu.VMEM((HEADS_PAD, 1), F32),
                            pltpu.VMEM((HEADS_PAD, KV_LORA), F32)]),
        out_shape=jax.ShapeDtypeStruct((bd, HEADS_PAD, KV_LORA), F32),
        compiler_params=pltpu.CompilerParams(dimension_semantics=("parallel", "arbitrary")),
        name="mla_decode",
    )(page_table, qlat, qrope, c_new.reshape(bd, 1, KV_LORA), kr_new.reshape(bd, 1, ROPE_DIM),
      *([pool_c] * pp), *([pool_kr] * pp))


def _pages_per_step(n_pages):
    pp = min(PAGES_PER_STEP, n_pages)
    assert n_pages % pp == 0 and pp % 2 == 0, (n_pages, pp)
    return pp


def _olat_proj_kernel(olat_ref, wuv_ref, o_ref):
    for j in range(N_HEADS // 2):
        o = (_mm(olat_ref[2 * j].astype(BF16), wuv_ref[2 * j])
             + _mm(olat_ref[2 * j + 1].astype(BF16), wuv_ref[2 * j + 1]))
        o_ref[:, j * LANES:(j + 1) * LANES] = o.astype(BF16)


def _olat_proj(olat_hm, wuvp):
    bd = olat_hm.shape[1]
    return pl.pallas_call(
        _olat_proj_kernel,
        out_shape=jax.ShapeDtypeStruct((bd, SELF_WIDTH), BF16),
        name="olat_proj",
    )(olat_hm, wuvp)


def _oproj_ln_kernel(os_ref, om_ref, x_ref, ws_ref, wm_ref, g_ref, b_ref, y_ref):
    mix = _mm(os_ref[...].astype(BF16), ws_ref[...]) + _mm(om_ref[...], wm_ref[...])
    y_ref[...] = _layer_norm(ALPHA * x_ref[...] + mix, g_ref[...], b_ref[...])


def _oproj_ln(o_self, o_mem, x, w_self, w_mem, g, b):
    rows = x.shape[0]
    tm = min(ROW_TILE, rows)
    ks = o_self.shape[1]
    row = lambda n: pl.BlockSpec((tm, n), lambda i: (i, 0))
    full = lambda a: pl.BlockSpec(a.shape, lambda i: (0, 0))
    return pl.pallas_call(
        _oproj_ln_kernel,
        grid=(rows // tm,),
        in_specs=[row(ks), row(MEM_WIDTH), row(D_MODEL), full(w_self), full(w_mem), full(g), full(b)],
        out_specs=row(D_MODEL),
        out_shape=jax.ShapeDtypeStruct((rows, D_MODEL), F32),
        compiler_params=pltpu.CompilerParams(dimension_semantics=("parallel",)),
        name="oproj_ln",
    )(o_self, o_mem, x, w_self, w_mem, g, b)


def _ffn_core(x, g, p1, p2, wu_ref, cw_ref, cb_ref, wd_ref):
    u = _mm(x, wu_ref[...])
    cw = cw_ref[...]
    gc = cb_ref[...] + cw[0:1, :] * p2
    gc = gc + cw[1:2, :] * p1
    gc = gc + cw[2:3, :] * g
    h = 0.5 * gc * (1.0 + lax.erf(gc * (2.0 ** -0.5))) * u
    return _mm(h.astype(BF16), wd_ref[...])


def _ffn_finish(f, x_ref, acc_ref, lg_ref, lb_ref, y_ref, part):
    nf = pl.num_programs(1)

    @pl.when(f == 0)
    def _():
        acc_ref[...] = part

    @pl.when(f > 0)
    def _():
        acc_ref[...] += part

    @pl.when(f == nf - 1)
    def _():
        y_ref[...] = _layer_norm(ALPHA * x_ref[...] + acc_ref[...], lg_ref[...], lb_ref[...])


def _ffn_p_kernel(x_ref, wu_ref, wg_ref, wd_ref, cw_ref, cb_ref, lg_ref, lb_ref,
                  y_ref, tail_ref, acc_ref, prev_ref, *, tiles_per_seq):
    i, f = pl.program_id(0), pl.program_id(1)
    x = x_ref[...].astype(BF16)
    g = _mm(x, wg_ref[...])
    tm = g.shape[0]

    @pl.when(i % tiles_per_seq == 0)
    def _():
        prev_ref[f] = jnp.zeros(prev_ref.shape[1:], F32)

    prev = prev_ref[f]
    row = lax.broadcasted_iota(I32, g.shape, 0)
    p1 = jnp.where(row == 0, prev[7:8, :], pltpu.roll(g, 1, 0))
    p2 = jnp.where(row == 0, prev[6:7, :], jnp.where(row == 1, prev[7:8, :], pltpu.roll(g, 2, 0)))
    tail = g[tm - 8:, :]
    prev_ref[f] = tail
    tail_ref[0] = tail
    _ffn_finish(f, x_ref, acc_ref, lg_ref, lb_ref, y_ref, _ffn_core(x, g, p1, p2, wu_ref, cw_ref, cb_ref, wd_ref))


def _ffn_s_kernel(x_ref, wu_ref, wg_ref, wd_ref, cw_ref, cb_ref, lg_ref, lb_ref, p1_ref, p2_ref,
                  y_ref, g_ref, acc_ref):
    f = pl.program_id(1)
    x = x_ref[...].astype(BF16)
    g = _mm(x, wg_ref[...])
    g_ref[...] = g
    _ffn_finish(f, x_ref, acc_ref, lg_ref, lb_ref, y_ref,
                _ffn_core(x, g, p1_ref[...], p2_ref[...], wu_ref, cw_ref, cb_ref, wd_ref))


def _ffn(x, w, seq=None, state=None):
    rows = x.shape[0]
    tm = min(ROW_TILE, seq if state is None else rows)
    tf = FF_TILE
    nf = D_FF // tf
    nr = rows // tm
    xs = pl.BlockSpec((tm, D_MODEL), lambda i, f: (i, 0))
    col = lambda r: pl.BlockSpec((r, tf), lambda i, f: (0, f))
    vec = pl.BlockSpec((1, D_MODEL), lambda i, f: (0, 0))
    in_specs = [xs, pl.BlockSpec((D_MODEL, tf), lambda i, f: (0, f)), pl.BlockSpec((D_MODEL, tf), lambda i, f: (0, f)),
                pl.BlockSpec((tf, D_MODEL), lambda i, f: (f, 0)), col(CONV_W), col(1), vec, vec]
    args = [x, w["wu"], w["wg"], w["wd"], w["cw"], w["cb"], w["lg"], w["lb"]]
    scratch = [pltpu.VMEM((tm, D_MODEL), F32)]
    if state is None:
        kern = functools.partial(_ffn_p_kernel, tiles_per_seq=seq // tm)
        out_specs = [xs, pl.BlockSpec((1, 8, tf), lambda i, f: (i, 0, f))]
        out_shape = [jax.ShapeDtypeStruct((rows, D_MODEL), F32), jax.ShapeDtypeStruct((nr, 8, D_FF), F32)]
        scratch.append(pltpu.VMEM((nf, 8, tf), F32))
    else:
        kern = _ffn_s_kernel
        rt = pl.BlockSpec((tm, tf), lambda i, f: (i, f))
        in_specs += [rt, rt]
        args += list(state)
        out_specs = [xs, rt]
        out_shape = [jax.ShapeDtypeStruct((rows, D_MODEL), F32), jax.ShapeDtypeStruct((rows, D_FF), F32)]
    return pl.pallas_call(
        kern,
        grid=(nr, nf),
        in_specs=in_specs,
        out_specs=out_specs,
        out_shape=out_shape,
        scratch_shapes=scratch,
        compiler_params=pltpu.CompilerParams(dimension_semantics=("arbitrary", "arbitrary")),
        name="ffn",
    )(*args)


_B_Q, _B_QS, _B_K, _B_KS, _B_V, _B_QM = 0, 768, 1536, 1792, 2048, 2304
KV_WIDTH = KV_HEADS * HEAD_DIM


def _proj_moba_kernel(x_ref, wb_ref, c_ref, s_ref,
                      qp_ref, qf_ref, kf_ref, vf_ref, kb_ref, vb_ref, qm_ref, km_ref):
    p = _mm(x_ref[...].astype(BF16), wb_ref[...])
    c, s = c_ref[...], s_ref[...]
    tm = p.shape[0]
    lane = lax.broadcasted_iota(I32, (tm, LANES), 1)
    for j in range(SELF_WIDTH // LANES):
        cs = slice(j * LANES, (j + 1) * LANES)
        q = p[:, _B_Q + j * LANES:_B_Q + (j + 1) * LANES] * c + p[:, _B_QS + j * LANES:_B_QS + (j + 1) * LANES] * s
        qf_ref[:, cs] = q
        qs = q * HEAD_SCALE
        qp_ref[:, (2 * j) * LANES:(2 * j + 1) * LANES] = jnp.where(lane < HALF, qs, 0.0).astype(BF16)
        qp_ref[:, (2 * j + 1) * LANES:(2 * j + 2) * LANES] = jnp.where(lane >= HALF, qs, 0.0).astype(BF16)
    for j in range(KV_WIDTH // LANES):
        cs = slice(j * LANES, (j + 1) * LANES)
        k = p[:, _B_K + j * LANES:_B_K + (j + 1) * LANES] * c + p[:, _B_KS + j * LANES:_B_KS + (j + 1) * LANES] * s
        kf_ref[:, cs] = k
        kb_ref[:, cs] = k.astype(BF16)
        for r in range(tm // MOBA_BLOCK):
            km_ref[r, :, cs] = jnp.sum(k[r * MOBA_BLOCK:(r + 1) * MOBA_BLOCK, :], axis=0, keepdims=True) * (1.0 / MOBA_BLOCK)
    v = p[:, _B_V:_B_V + KV_WIDTH]
    vf_ref[...] = v
    vb_ref[...] = v.astype(BF16)
    qm_ref[...] = p[:, _B_QM:_B_QM + MEM_WIDTH].astype(BF16)


def _proj_moba(x, wb, tabs, seq):
    rows = x.shape[0]
    tm = min(ROW_TILE, seq)
    assert tm % MOBA_BLOCK == 0
    nt = seq // tm
    tab = pl.BlockSpec((tm, LANES), lambda i: (i % nt, 0))
    row = lambda n: pl.BlockSpec((tm, n), lambda i: (i, 0))
    bpt = tm // MOBA_BLOCK
    return pl.pallas_call(
        _proj_moba_kernel,
        grid=(rows // tm,),
        in_specs=[row(D_MODEL), pl.BlockSpec(wb.shape, lambda i: (0, 0)), tab, tab],
        out_specs=[row(2 * SELF_WIDTH), row(SELF_WIDTH), row(KV_WIDTH), row(KV_WIDTH), row(KV_WIDTH), row(KV_WIDTH),
                   row(MEM_WIDTH), pl.BlockSpec((bpt, 1, KV_WIDTH), lambda i: (i, 0, 0))],
        out_shape=[jax.ShapeDtypeStruct((rows, 2 * SELF_WIDTH), BF16), jax.ShapeDtypeStruct((rows, SELF_WIDTH), F32),
                   jax.ShapeDtypeStruct((rows, KV_WIDTH), F32), jax.ShapeDtypeStruct((rows, KV_WIDTH), F32),
                   jax.ShapeDtypeStruct((rows, KV_WIDTH), BF16), jax.ShapeDtypeStruct((rows, KV_WIDTH), BF16),
                   jax.ShapeDtypeStruct((rows, MEM_WIDTH), BF16),
                   jax.ShapeDtypeStruct((rows // MOBA_BLOCK, 1, KV_WIDTH), F32)],
        compiler_params=pltpu.CompilerParams(dimension_semantics=("parallel",)),
        name="proj_moba",
    )(x, wb, tabs["c"], tabs["s"])


_S_QW = N_HEADS * KV_WIDTH
_S_K = 2 * _S_QW


def _proj_moba_s_kernel(x_ref, wbs_ref, c_ref, s_ref, qe_ref, kf_ref, vf_ref, qm_ref):
    p = _mm(x_ref[...].astype(BF16), wbs_ref[...])
    c, s = c_ref[...], s_ref[...]
    for h in range(N_HEADS):
        qe_ref[:, h * KV_WIDTH:(h + 1) * KV_WIDTH] = (p[:, h * KV_WIDTH:(h + 1) * KV_WIDTH] * c
                                                       + p[:, _S_QW + h * KV_WIDTH:_S_QW + (h + 1) * KV_WIDTH] * s)
    for h in range(N_HEADS, HEADS_PAD):
        qe_ref[:, h * KV_WIDTH:(h + 1) * KV_WIDTH] = jnp.zeros((p.shape[0], KV_WIDTH), F32)
    kf_ref[...] = p[:, _S_K:_S_K + KV_WIDTH] * c + p[:, _S_K + KV_WIDTH:_S_K + 2 * KV_WIDTH] * s
    vf_ref[...] = p[:, _S_K + 2 * KV_WIDTH:_S_K + 3 * KV_WIDTH]
    qm_ref[...] = p[:, _S_K + 3 * KV_WIDTH:_S_K + 3 * KV_WIDTH + MEM_WIDTH].astype(BF16)


def _proj_moba_s(x, wbs, tabs):
    bd = x.shape[0]
    return pl.pallas_call(
        _proj_moba_s_kernel,
        out_shape=[jax.ShapeDtypeStruct((bd, HEADS_PAD * KV_WIDTH), F32), jax.ShapeDtypeStruct((bd, KV_WIDTH), F32),
                   jax.ShapeDtypeStruct((bd, KV_WIDTH), F32), jax.ShapeDtypeStruct((bd, MEM_WIDTH), BF16)],
        name="proj_moba_s",
    )(x, wbs, tabs["c"], tabs["s"])


def _moba_attn_kernel(qp_ref, qf_ref, k_ref, v_ref, km_ref, o_ref, *, nb):
    i = pl.program_id(2)
    t = qp_ref.shape[0]
    lane = lax.broadcasted_iota(I32, (t, LANES), 1)
    causal = lax.broadcasted_iota(I32, (t, t), 1) <= lax.broadcasted_iota(I32, (t, t), 0)
    km = km_ref[0]
    if nb < LANES:
        km = jnp.concatenate([km, jnp.zeros((LANES - nb, LANES), F32)], axis=0)
    qf = qf_ref[...]
    past = lane < i
    outs = []
    for half in range(2):
        in_half = (lane >= HALF) if half else (lane < HALF)
        gate = _mm_nt(jnp.where(in_half, qf, 0.0), km, precision=lax.Precision.HIGHEST)
        gate = jnp.where(past, gate, -jnp.inf)
        rank = jnp.zeros((t, LANES), F32)
        for j in range(nb):
            cj = gate[:, j:j + 1]
            rank = rank + jnp.where((cj > gate) | ((cj == gate) & (j < lane)), 1.0, 0.0)
        sel = jnp.where((rank < MOBA_TOPK) & past & (gate > -jnp.inf), 1.0, 0.0)
        q = qp_ref[:, half * LANES:(half + 1) * LANES]

        def tile(j):
            r0 = pl.multiple_of(j * t, t)
            return _mm_nt(q, k_ref[pl.ds(r0, t), :]), v_ref[pl.ds(r0, t), :]

        s, v = tile(i)
        m, l, p = _softmax_start(jnp.where(causal, s, NEG))
        acc = _mm(p.astype(BF16), v)

        def body(j, carry):
            m, l, acc = carry
            s, v = tile(j)
            chosen = jnp.sum(jnp.where(lane == j, sel, 0.0), axis=1, keepdims=True) > 0.5
            m, l, a, p = _softmax_step(jnp.where(chosen, s, NEG), m, l)
            return m, l, a * acc + _mm(p.astype(BF16), v)

        m, l, acc = lax.fori_loop(0, i, body, (m, l, acc))
        outs.append(acc / l)
    o_ref[...] = jnp.where(lane < HALF, outs[0], outs[1]).astype(BF16)


def _moba_attn(qp, qf, kb, vb, kmean, batch, seq):
    rows = qp.shape[0]
    t = MOBA_BLOCK
    nb = seq // t
    npair = SELF_WIDTH // LANES
    per_kvb = npair // (KV_WIDTH // LANES)
    return pl.pallas_call(
        functools.partial(_moba_attn_kernel, nb=nb),
        grid=(batch, npair, nb),
        in_specs=[pl.BlockSpec((t, 2 * LANES), lambda b, j, i: (b * nb + i, j)),
                  pl.BlockSpec((t, LANES), lambda b, j, i: (b * nb + i, j)),
                  pl.BlockSpec((seq, LANES), lambda b, j, i: (b, j // per_kvb)),
                  pl.BlockSpec((seq, LANES), lambda b, j, i: (b, j // per_kvb)),
                  pl.BlockSpec((1, nb, LANES), lambda b, j, i: (b, 0, j // per_kvb))],
        out_specs=pl.BlockSpec((t, LANES), lambda b, j, i: (b * nb + i, j)),
        out_shape=jax.ShapeDtypeStruct((rows, SELF_WIDTH), BF16),
        compiler_params=pltpu.CompilerParams(dimension_semantics=("parallel", "parallel", "arbitrary")),
        name="moba_attn",
    )(qp, qf, kb, vb, kmean)


def _moba_select_kernel(pt_ref, qe_ref, *rest, pages, nblk):
    del pt_ref
    k_refs = rest[:pages]
    sel_ref, km_ref = rest[pages:]
    step = pl.program_id(1)
    per_step = pages // 2

    @pl.when(step == 0)
    def _():
        km_ref[...] = jnp.zeros(km_ref.shape, F32)

    sub = lax.broadcasted_iota(I32, (per_step, KV_WIDTH), 0)
    tile = jnp.zeros((per_step, KV_WIDTH), F32)
    for r in range(per_step):
        tot = (jnp.sum(k_refs[2 * r][0], axis=0, keepdims=True)
               + jnp.sum(k_refs[2 * r + 1][0], axis=0, keepdims=True)) * (1.0 / MOBA_BLOCK)
        tile = jnp.where(sub == r, tot, tile)
    km_ref[pl.ds(pl.multiple_of(step * per_step, per_step), per_step), :] = tile

    @pl.when(step == pl.num_programs(1) - 1)
    def _():
        gate = _mm_nt(qe_ref[0], km_ref[...], precision=lax.Precision.HIGHEST)
        lane = lax.broadcasted_iota(I32, gate.shape, 1)
        gate = jnp.where(lane < nblk, gate, -jnp.inf)
        out = jnp.zeros(sel_ref.shape[1:], I32)
        olane = lax.broadcasted_iota(I32, out.shape, 1)
        for t in range(MOBA_TOPK):
            best = jnp.max(gate, axis=1, keepdims=True)
            idx = jnp.min(jnp.where(gate == best, lane, nblk), axis=1, keepdims=True)
            out = jnp.where(olane == t, idx, out)
            gate = jnp.where(lane == idx, -jnp.inf, gate)
        sel_ref[0] = out


def _moba_select(page_table, qe, pool_k):
    bd, n_pages = page_table.shape
    page = pool_k.shape[1]
    assert 2 * page == MOBA_BLOCK
    pp = _pages_per_step(n_pages)
    nblk = n_pages // 2
    assert (pp // 2) % 8 == 0 and MOBA_TOPK <= nblk <= LANES
    km_rows = LANES

    def pmap(b, s, pt, i):
        return (pt[b, s * pp + i], 0, 0)

    return pl.pallas_call(
        functools.partial(_moba_select_kernel, pages=pp, nblk=nblk),
        grid_spec=pltpu.PrefetchScalarGridSpec(
            num_scalar_prefetch=1, grid=(bd, n_pages // pp),
            in_specs=[pl.BlockSpec((1, HEADS_PAD, KV_WIDTH), lambda b, s, pt: (b, 0, 0))]
            + [pl.BlockSpec((1, page, KV_WIDTH), functools.partial(pmap, i=i)) for i in range(pp)],
            out_specs=pl.BlockSpec((1, HEADS_PAD, LANES), lambda b, s, pt: (b, 0, 0)),
            scratch_shapes=[pltpu.VMEM((km_rows, KV_WIDTH), F32)]),
        out_shape=jax.ShapeDtypeStruct((bd, HEADS_PAD, LANES), I32),
        compiler_params=pltpu.CompilerParams(dimension_semantics=("parallel", "arbitrary")),
        name="moba_select",
    )(page_table, qe, *([pool_k] * pp))


def _moba_decode_kernel(pt_ref, sel_ref, qe_ref, kn_ref, vn_ref, *rest):
    del pt_ref, sel_ref
    npg = 2 * MOBA_TOPK
    k_refs, v_refs = rest[:npg], rest[npg:2 * npg]
    o_ref = rest[2 * npg]
    h = pl.program_id(1)
    q = qe_ref[0, pl.ds(h, 1), :]
    lane = lax.broadcasted_iota(I32, q.shape, 1)
    qb = jnp.broadcast_to((q * HEAD_SCALE).astype(BF16), (8, KV_WIDTH))
    ss = [_mm_nt(qb, k_refs[i][0].astype(BF16)) for i in range(npg)]
    s = jnp.concatenate(ss, axis=1)
    kn = kn_ref[0].astype(BF16).astype(F32)
    vn = vn_ref[0].astype(BF16).astype(F32)
    s_new = jnp.sum(qb[0:1].astype(F32) * kn, axis=1, keepdims=True)
    m = jnp.maximum(jnp.max(s, axis=1, keepdims=True), s_new)
    p = jnp.exp(s - m)
    p_new = jnp.exp(s_new - m)
    l = jnp.sum(p, axis=1, keepdims=True) + p_new
    page = k_refs[0].shape[1]
    pv = p_new.astype(BF16).astype(F32) * vn
    for i in range(npg):
        pv = pv + _mm(p[:, i * page:(i + 1) * page].astype(BF16), v_refs[i][0].astype(BF16))
    o = (pv / l)[0:1]
    keep = (lane // HEAD_DIM) == (h // (N_HEADS // KV_HEADS))
    o_ref[0, pl.ds(h, 1), :] = jnp.where(keep, o, 0.0)


def _moba_decode(page_table, sel, qe, k_new, v_new, pool_k, pool_v):
    bd, n_pages = page_table.shape
    page = pool_k.shape[1]

    def pmap(b, h, pt, sl, t, half):
        return (pt[b, 2 * sl[(b * N_HEADS + h) * MOBA_TOPK + t] + half], 0, 0)

    pages = [pl.BlockSpec((1, page, KV_WIDTH), functools.partial(pmap, t=t, half=half))
             for t in range(MOBA_TOPK) for half in range(2)]
    fixed = lambda r: pl.BlockSpec((1, r, KV_WIDTH), lambda b, h, pt, sl: (b, 0, 0))
    return pl.pallas_call(
        _moba_decode_kernel,
        grid_spec=pltpu.PrefetchScalarGridSpec(
            num_scalar_prefetch=2, grid=(bd, N_HEADS),
            in_specs=[fixed(HEADS_PAD), fixed(1), fixed(1)] + pages + pages,
            out_specs=fixed(N_HEADS)),
        out_shape=jax.ShapeDtypeStruct((bd, N_HEADS, KV_WIDTH), F32),
        compiler_params=pltpu.CompilerParams(dimension_semantics=("parallel", "arbitrary")),
        name="moba_decode",
    )(page_table, sel, qe, k_new.reshape(bd, 1, KV_WIDTH), v_new.reshape(bd, 1, KV_WIDTH),
      *([pool_k] * (2 * MOBA_TOPK)), *([pool_v] * (2 * MOBA_TOPK)))


def _rope_cos_sin(pos, d):
    inv = ROPE_THETA ** (-jnp.arange(0, d, 2, dtype=F32) / d)
    ang = pos.astype(F32)[:, None] * inv[None, :]
    return jnp.cos(ang), jnp.sin(ang)


def _mla_tables(pos):
    n = pos.shape[0]
    cos, sin = _rope_cos_sin(pos, ROPE_DIM)
    one, zero = jnp.ones((n, NOPE_DIM), F32), jnp.zeros((n, NOPE_DIM), F32)
    pad = jnp.zeros((n, LANES - NOPE_DIM - ROPE_DIM), F32)
    padk = jnp.zeros((n, LANES - ROPE_DIM), F32)
    return {"cq": MLA_SCALE * jnp.concatenate([one, cos, cos, pad], 1),
            "sq": MLA_SCALE * jnp.concatenate([zero, sin, sin, pad], 1),
            "ckr": jnp.concatenate([cos, cos, padk], 1), "skr": jnp.concatenate([sin, sin, padk], 1)}


def _moba_tables(pos, width):
    cos, sin = _rope_cos_sin(pos, HEAD_DIM)
    reps = 2 * (width // HEAD_DIM)
    return {"c": jnp.concatenate([cos] * reps, 1), "s": jnp.concatenate([sin] * reps, 1)}


def _swap_halves(w):
    d = w.shape[-1] // 2
    return jnp.concatenate([-w[..., d:], w[..., :d]], -1)


def _prep_mla(w_in, g_q, w_q_b, g_kv, w_uk, w_uv):
    d = w_in.shape[0]
    z = lambda *s: jnp.zeros(s, F32)
    kr_w = w_in[:, Q_LORA + KV_LORA:Q_LORA + KV_LORA + ROPE_DIM]
    qm_w = w_in[:, Q_LORA + KV_LORA + ROPE_DIM:] * HEAD_SCALE
    wa = jnp.concatenate([w_in[:, :Q_LORA + KV_LORA], kr_w, z(d, LANES - ROPE_DIM),
                          _swap_halves(kr_w), z(d, LANES - ROPE_DIM), qm_w], 1)
    wq = w_q_b.reshape(Q_LORA, N_HEADS, NOPE_DIM + ROPE_DIM)
    nope, rope = wq[..., :NOPE_DIM], wq[..., NOPE_DIM:]
    hpad = z(Q_LORA, N_HEADS, LANES - NOPE_DIM - ROPE_DIM)
    w1 = jnp.concatenate([nope, rope, hpad], -1).reshape(Q_LORA, MLA_QW)
    w2 = jnp.concatenate([jnp.zeros_like(nope), _swap_halves(rope), hpad], -1).reshape(Q_LORA, MLA_QW)
    uk = jnp.concatenate([w_uk, z(KV_LORA, N_HEADS, LANES - NOPE_DIM)], -1)
    place = jnp.concatenate([z(ROPE_DIM, NOPE_DIM), jnp.eye(ROPE_DIM, dtype=F32), z(ROPE_DIM, LANES - NOPE_DIM - ROPE_DIM)], 1)
    place = jnp.broadcast_to(place[:, None, :], (ROPE_DIM, N_HEADS, LANES))
    wk = jnp.concatenate([uk, place, z(LANES - ROPE_DIM, N_HEADS, LANES)], 0).reshape(KV_LORA + LANES, MLA_QW)
    wukt = jnp.concatenate([jnp.transpose(w_uk, (1, 2, 0)), z(N_HEADS, LANES - NOPE_DIM, KV_LORA)], 1)
    psel = jnp.concatenate([z(NOPE_DIM, LANES), jnp.eye(ROPE_DIM, LANES, dtype=F32),
                            z(LANES - NOPE_DIM - ROPE_DIM, LANES)], 0)
    uv = jnp.transpose(w_uv, (1, 0, 2))
    even = (jnp.arange(N_HEADS) % 2 == 0)[:, None, None]
    wuvp = jnp.where(even, jnp.concatenate([uv, jnp.zeros_like(uv)], -1), jnp.concatenate([jnp.zeros_like(uv), uv], -1))
    b = lambda a: a.astype(BF16)
    return {"wa": b(wa), "wq": b(jnp.concatenate([w1, w2], 1)), "wk": b(wk),
            "wv": b(w_uv.reshape(KV_LORA, SELF_WIDTH)), "wukt": b(wukt), "psel": b(psel), "wuvp": b(wuvp),
            "gq": g_q.reshape(1, Q_LORA), "gkv": g_kv.reshape(1, KV_LORA)}


def _prep_moba(w_in):
    d = w_in.shape[0]
    nq, nk = N_HEADS * HEAD_DIM, KV_WIDTH
    perm = jnp.array(MOBA_PERM)
    wq = w_in[:, :nq].reshape(d, N_HEADS, HEAD_DIM)
    wk = w_in[:, nq:nq + nk].reshape(d, KV_HEADS, HEAD_DIM)
    wv = w_in[:, nq + nk:nq + 2 * nk]
    wqm = w_in[:, nq + 2 * nk:] * HEAD_SCALE
    wqp = wq[:, perm, :]
    flat = lambda a: a.reshape(d, -1)
    wb = jnp.concatenate([flat(wqp), flat(_swap_halves(wqp)), flat(wk), flat(_swap_halves(wk)), wv, wqm], 1)
    kv_of = jnp.arange(N_HEADS) // (N_HEADS // KV_HEADS)
    onehot = (kv_of[:, None] == jnp.arange(KV_HEADS)[None, :]).astype(F32)
    expand = lambda a: (a[:, :, None, :] * onehot[None, :, :, None]).reshape(d, N_HEADS * KV_WIDTH)
    wbs = jnp.concatenate([expand(wq), expand(_swap_halves(wq)), flat(wk), flat(_swap_halves(wk)), wv, wqm], 1)
    return wb.astype(BF16), wbs.astype(BF16)


def _prep_wo(w_o):
    w0, w1 = w_o[0], w_o[1]
    perm = jnp.array(MOBA_PERM)
    w1_self = w1[:SELF_WIDTH].reshape(N_HEADS, V_DIM, D_MODEL)
    kv_of = jnp.arange(N_HEADS) // (N_HEADS // KV_HEADS)
    onehot = (kv_of[:, None] == jnp.arange(KV_HEADS)[None, :]).astype(F32)
    w1_exp = (w1_self[:, None, :, :] * onehot[:, :, None, None]).reshape(N_HEADS * KV_WIDTH, D_MODEL)
    b = lambda a: a.astype(BF16)
    return {"s0": b(w0[:SELF_WIDTH]), "m0": b(w0[SELF_WIDTH:]),
            "s1": b(w1_self[perm].reshape(SELF_WIDTH, D_MODEL)), "m1": b(w1[SELF_WIDTH:]), "s1x": b(w1_exp)}


def _prep_ffn(i, w_up, conv_w, conv_b, w_down, ln2_g, ln2_b):
    return {"wu": w_up[i][:, :D_FF].astype(BF16), "wg": w_up[i][:, D_FF:].astype(BF16), "wd": w_down[i].astype(BF16),
            "cw": conv_w[i], "cb": conv_b[i].reshape(1, D_FF),
            "lg": ln2_g[i].reshape(1, D_MODEL), "lb": ln2_b[i].reshape(1, D_MODEL)}


def kernel(x_prompt, x_sample, cache_mla_ckv, cache_mla_krope, cache_moba_k, cache_moba_v, cache_mem_k, cache_mem_v, state_conv, page_table, mem_prompt, w_in_a, g_q, w_q_b, g_kv, w_uk, w_uv, w_in_b, w_mem_k, w_mem_v, w_o, ln1_g, ln1_b, w_up, conv_w, conv_b, w_down, ln2_g, ln2_b):
    batch, seq, d = x_prompt.shape
    bd, dec_seq, _ = x_sample.shape
    n_pool, page = cache_mla_ckv.shape[1], cache_mla_ckv.shape[2]
    n_pages = page_table.shape[1]
    mem_len = mem_prompt.shape[1]
    assert d == D_MODEL and dec_seq == 1 and w_o.shape[0] == DEPTH and w_in_a.shape[0] == 1 and w_in_b.shape[0] == 1
    assert seq % MOBA_BLOCK == 0 and (n_pages * page) % MOBA_BLOCK == 0
    rows = batch * seq
    past_len = n_pages * page
    pos_p = jnp.arange(seq, dtype=I32)
    pos_s = jnp.full((bd,), past_len, I32)

    xp = x_prompt.reshape(rows, d)
    xs = x_sample.reshape(bd, d)
    wm = jnp.concatenate([w_mem_k[0], w_mem_v[0], w_mem_k[1], w_mem_v[1]], 1).astype(BF16)
    wmla = _prep_mla(w_in_a[0], g_q[0], w_q_b[0], g_kv[0], w_uk[0], w_uv[0])
    wb, wbs = _prep_moba(w_in_b[0])
    wo = _prep_wo(w_o)
    ffn_w = [_prep_ffn(i, w_up, conv_w, conv_b, w_down, ln2_g, ln2_b) for i in range(DEPTH)]
    ln1 = [(ln1_g[i].reshape(1, d), ln1_b[i].reshape(1, d)) for i in range(DEPTH)]
    cmk = cache_mem_k.reshape(DEPTH, bd, mem_len, MEM_WIDTH)
    cmv = cache_mem_v.reshape(DEPTH, bd, mem_len, MEM_WIDTH)
    tails = seq // min(ROW_TILE, seq)

    memk, memv, memk_b, memv_b = _mem_proj(mem_prompt.reshape(batch * mem_len, d), wm)

    q0, k0, v0, qm0, ckv_p, krope_p = _proj_mla(xp, wmla, _mla_tables(pos_p), seq)
    o_self = _mla_attn(q0, k0, v0, batch, seq)
    o_mem = _mem_attn(qm0, memk_b, memv_b, 0, batch, mem_len)
    xp = _oproj_ln(o_self, o_mem, xp, wo["s0"], wo["m0"], *ln1[0])
    xp, tail_p0 = _ffn(xp, ffn_w[0], seq=seq)

    qlat, qrope, qms, ckv_s, krope_s = _proj_mla_s(xs, wmla, _mla_tables(pos_s))
    olat = _mla_decode(page_table, jnp.transpose(qlat, (1, 0, 2)), jnp.transpose(qrope, (1, 0, 2)), ckv_s, krope_s,
                       cache_mla_ckv.reshape(n_pool, page, KV_LORA), cache_mla_krope.reshape(n_pool, page, ROPE_DIM))
    o_self_s = _olat_proj(jnp.transpose(olat, (1, 0, 2)), wmla["wuvp"])
    o_mem_s = _mem_attn_s(qms, cmk, cmv, 0)
    xs = _oproj_ln(o_self_s, o_mem_s, xs, wo["s0"], wo["m0"], *ln1[0])
    xs, g_s0 = _ffn(xs, ffn_w[0], state=(state_conv[0, :, 1, :], state_conv[0, :, 0, :]))

    qp, qf, mobak_p, mobav_p, kb, vb, qm1, kmean = _proj_moba(xp, wb, _moba_tables(pos_p, LANES), seq)
    o_self = _moba_attn(qp, qf, kb, vb, kmean.reshape(batch, seq // MOBA_BLOCK, KV_WIDTH), batch, seq)
    o_mem = _mem_attn(qm1, memk_b, memv_b, 1, batch, mem_len)
    xp = _oproj_ln(o_self, o_mem, xp, wo["s1"], wo["m1"], *ln1[1])
    xp, tail_p1 = _ffn(xp, ffn_w[1], seq=seq)

    qe, mobak_s, mobav_s, qms = _proj_moba_s(xs, wbs, _moba_tables(pos_s, KV_WIDTH))
    qe = qe.reshape(bd, HEADS_PAD, KV_WIDTH)
    pool_k = cache_moba_k.reshape(n_pool, page, KV_WIDTH)
    pool_v = cache_moba_v.reshape(n_pool, page, KV_WIDTH)
    sel = _moba_select(page_table, qe, pool_k)
    sel = sel[:, :N_HEADS, :MOBA_TOPK].reshape(bd * N_HEADS * MOBA_TOPK)
    o_exp = _moba_decode(page_table, sel, qe, mobak_s, mobav_s, pool_k, pool_v)
    o_mem_s = _mem_attn_s(qms, cmk, cmv, 1)
    xs = _oproj_ln(o_exp.reshape(bd, N_HEADS * KV_WIDTH), o_mem_s, xs, wo["s1x"], wo["m1"], *ln1[1])
    xs, g_s1 = _ffn(xs, ffn_w[1], state=(state_conv[1, :, 1, :], state_conv[1, :, 0, :]))

    def conv_rows_p(tail):
        last = tail.reshape(batch, tails, 8, D_FF)[:, tails - 1]
        return last[:, 8 - (CONV_W - 1):, :]

    conv_p = jnp.stack([conv_rows_p(tail_p0), conv_rows_p(tail_p1)])
    conv_s = jnp.stack([jnp.stack([state_conv[0, :, 1, :], g_s0], 1), jnp.stack([state_conv[1, :, 1, :], g_s1], 1)])
    kv5 = lambda a, n: a.reshape(1, n, -1, KV_HEADS, HEAD_DIM)
    return (xp.reshape(batch, seq, d), xs.reshape(bd, 1, d),
            ckv_p.reshape(1, batch, seq, KV_LORA), krope_p.reshape(1, batch, seq, ROPE_DIM),
            kv5(mobak_p, batch), kv5(mobav_p, batch),
            memk.reshape(DEPTH, batch, mem_len, MEM_HEADS, MEM_HEAD_DIM),
            memv.reshape(DEPTH, batch, mem_len, MEM_HEADS, MEM_HEAD_DIM),
            conv_p,
            ckv_s.reshape(1, bd, 1, KV_LORA), krope_s.reshape(1, bd, 1, ROPE_DIM),
            kv5(mobak_s, bd), kv5(mobav_s, bd),
            conv_s)
```
